```python
import jax, jax.numpy as jnp
from jax import lax
import numpy as np

D_MODEL = 1024
BATCH = 8
SEQ = 2048
DEPTH = 1
DEC_BATCH = 128
DEC_SEQ = 1
PAST_LEN = 16384
PAGE_SIZE = 128

MLSTM_HEADS = 4
MLSTM_DK = D_MODEL // 8
MLSTM_DV = D_MODEL // 8
D_A = MLSTM_HEADS * MLSTM_DV
QK_A = MLSTM_HEADS * MLSTM_DK
MLSTM_CHUNK = 128
POOL_WINDOWS = (2, 4, 8, 16)
N_POOL_GROUPS = 4
D_B = D_MODEL // 2
POOL_GROUP = D_B // N_POOL_GROUPS
POOL_BUF = max(POOL_WINDOWS) - 1
PEER_HEADS = 8
PEER_DKEY = D_MODEL // 4
PEER_HALF = PEER_DKEY // 2
PEER_N_KEYS = 128
PEER_N_EXPERTS = PEER_N_KEYS ** 2
PEER_TOPK = 16
PEER_TOKEN_BLOCK = 128
OFF_Q = 0
OFF_K = OFF_Q + QK_A
OFF_V = OFF_K + QK_A
OFF_O = OFF_V + D_A
OFF_I = OFF_O + D_A
OFF_F = OFF_I + MLSTM_HEADS
OFF_U = OFF_F + MLSTM_HEADS
OFF_GA = OFF_U + D_B
OFF_GB = OFF_GA + D_MODEL
N_IN = OFF_GB + D_MODEL
DEEPNORM_ALPHA = (2 * DEPTH) ** 0.25
DEEPNORM_BETA = (8 * DEPTH) ** -0.25
LN_EPS = 1e-5

kernel_name = 'hybrid_mlstm_pool_peer_step'


def layer_norm(x):
    xf = x.astype(jnp.float32)
    mu = jnp.mean(xf, axis=-1, keepdims=True)
    var = jnp.mean(jnp.square(xf - mu), axis=-1, keepdims=True)
    return (xf - mu) * lax.rsqrt(var + LN_EPS)


def mlstm_chunk(carry, inp):
    C, n, m = carry
    q, k, v, ig, lf = inp
    T = q.shape[2]
    b = jnp.cumsum(lf, axis=-1)
    causal = jnp.tril(jnp.ones((T, T), dtype=bool))
    log_d = b[..., :, None] - b[..., None, :] + ig[..., None, :]
    log_d = jnp.where(causal, log_d, -jnp.inf)
    log_g = b + m[..., None]
    m_t = jnp.maximum(log_g, jnp.max(log_d, axis=-1))
    w = jnp.exp(log_d - m_t[..., None]) * jnp.einsum('bhtd,bhsd->bhts', q, k)
    w_state = jnp.exp(log_g - m_t)
    num = jnp.einsum('bhts,bhsv->bhtv', w, v) + w_state[..., None] * jnp.einsum('bhtd,bhdv->bhtv', q, C)
    den = jnp.sum(w, axis=-1) + w_state * jnp.einsum('bhtd,bhd->bht', q, n)
    h = num / jnp.maximum(jnp.abs(den), jnp.exp(-m_t))[..., None]
    m_new = m_t[..., -1]
    decay_s = jnp.exp(b[..., -1:] - b + ig - m_new[..., None])
    decay_c = jnp.exp(b[..., -1] + m - m_new)
    C_new = decay_c[..., None, None] * C + jnp.einsum('bhs,bhsd,bhsv->bhdv', decay_s, k, v)
    n_new = decay_c[..., None] * n + jnp.einsum('bhs,bhsd->bhd', decay_s, k)
    return (C_new, n_new, m_new), h


def mlstm_scan(q, k, v, ig, lf, state):
    T = q.shape[2]
    L = MLSTM_CHUNK if T % MLSTM_CHUNK == 0 else T
    nc = T // L

    def split(a):
        a = a.astype(jnp.float32)
        a = a.reshape(a.shape[:2] + (nc, L) + a.shape[3:])
        return jnp.moveaxis(a, 2, 0)

    state = (state[0].astype(jnp.float32), state[1].astype(jnp.float32), state[2].astype(jnp.float32))
    state, h = lax.scan(mlstm_chunk, state, (split(q), split(k), split(v), split(ig), split(lf)))
    h = jnp.moveaxis(h, 0, 2).reshape(q.shape[:3] + (v.shape[-1],))
    return h, state


def multiscale_pool(u_ext, pos0, T):
    Bn = u_ext.shape[0]
    uf = u_ext.astype(jnp.float32).reshape(Bn, POOL_BUF + T, N_POOL_GROUPS, POOL_GROUP)
    cs = jnp.pad(jnp.cumsum(uf, axis=1), ((0, 0), (1, 0), (0, 0), (0, 0)))
    pos = pos0 + jnp.arange(T)
    end = POOL_BUF + 1
    outs = []
    for g, w in enumerate(POOL_WINDOWS):
        win_sum = cs[:, end:end + T, g] - cs[:, end - w:end - w + T, g]
        cnt = jnp.minimum(w, pos + 1).astype(jnp.float32)
        outs.append(win_sum / cnt[None, :, None] - uf[:, POOL_BUF:, g])
    return jnp.stack(outs, axis=2).reshape(Bn, T, D_B)


def token_mixer(h, C0, n0, m0, pool_prefix, pos0, w_in, b_in, b_fgate, gn_gain, w_pool, pool_scale,
                w_branch_a, w_branch_b, w_out):
    Bn, T, _ = h.shape
    z = h @ w_in + b_in

    def heads(a, d):
        return a.reshape(Bn, T, MLSTM_HEADS, d).transpose(0, 2, 1, 3)

    q = heads(z[..., OFF_Q:OFF_K], MLSTM_DK)
    k = heads(z[..., OFF_K:OFF_V], MLSTM_DK) * (MLSTM_DK ** -0.5)
    v = heads(z[..., OFF_V:OFF_O], MLSTM_DV)
    o = jax.nn.sigmoid(z[..., OFF_O:OFF_I])
    ig = z[..., OFF_I:OFF_F].transpose(0, 2, 1)
    lf = jax.nn.log_sigmoid(z[..., OFF_F:OFF_U] + b_fgate).transpose(0, 2, 1)
    hA, mstate = mlstm_scan(q, k, v, ig, lf, (C0, n0, m0))
    hA = layer_norm(hA) * gn_gain.reshape(MLSTM_HEADS, 1, MLSTM_DV)
    hA = hA.transpose(0, 2, 1, 3).reshape(Bn, T, D_A) * o
    u = z[..., OFF_U:OFF_GA]
    u_ext = jnp.concatenate([pool_prefix.astype(u.dtype), u], axis=1)
    pooled = multiscale_pool(u_ext, pos0, T)
    pB = jnp.einsum('btgc,gcd->btgd', pooled.reshape(Bn, T, N_POOL_GROUPS, POOL_GROUP), w_pool)
    pB = pB.reshape(Bn, T, D_B) * pool_scale
    gA = jax.nn.sigmoid(z[..., OFF_GA:OFF_GB])
    gB = jax.nn.sigmoid(z[..., OFF_GB:N_IN])
    merged = gA * (hA @ w_branch_a) + gB * (pB @ w_branch_b)
    return merged @ w_out, mstate, u_ext[:, -POOL_BUF:]


def peer_ffn(h, w_peer_q, peer_subkeys, peer_u, peer_v):
    Bn, T, D = h.shape
    x = h.reshape(Bn * T, D)
    N = x.shape[0]
    q = (x @ w_peer_q).astype(jnp.float32).reshape(N, PEER_HEADS, 2, PEER_HALF)
    s = jnp.einsum('nhpd,pkd->nhpk', q, peer_subkeys.astype(jnp.float32))
    sv, si = lax.top_k(s, PEER_TOPK)
    cand = (sv[:, :, 0, :, None] + sv[:, :, 1, None, :]).reshape(N, PEER_HEADS, PEER_TOPK * PEER_TOPK)
    cand_id = (si[:, :, 0, :, None] * PEER_N_KEYS + si[:, :, 1, None, :]).reshape(N, PEER_HEADS, PEER_TOPK * PEER_TOPK)
    top_s, top_pos = lax.top_k(cand, PEER_TOPK)
    ids = jnp.take_along_axis(cand_id, top_pos, axis=-1).reshape(N, PEER_HEADS * PEER_TOPK)
    gates = jax.nn.softmax(top_s, axis=-1).reshape(N, PEER_HEADS * PEER_TOPK)
    nb = -(-N // PEER_TOKEN_BLOCK)
    pad = nb * PEER_TOKEN_BLOCK - N
    xb = jnp.pad(x, ((0, pad), (0, 0))).reshape(nb, PEER_TOKEN_BLOCK, D)
    idb = jnp.pad(ids, ((0, pad), (0, 0))).reshape(nb, PEER_TOKEN_BLOCK, -1)
    gb = jnp.pad(gates, ((0, pad), (0, 0))).reshape(nb, PEER_TOKEN_BLOCK, -1)

    def block(args):
        xs, ids_b, g_b = args
        act = jax.nn.gelu(jnp.einsum('ned,nd->ne', peer_u[ids_b], xs), approximate=False)
        return jnp.einsum('ne,ned->nd', g_b * act, peer_v[ids_b])

    y = lax.map(block, (xb, idb, gb)).reshape(nb * PEER_TOKEN_BLOCK, D)[:N]
    return y.reshape(Bn, T, D)


def decoder_layer(x, c, C0, n0, m0, pool_prefix, pos0, w_mod, b_mod, w_in, b_in, b_fgate, gn_gain, w_pool,
                  pool_scale, w_branch_a, w_branch_b, w_out, ln1_g, ln1_b, w_peer_q, peer_subkeys, peer_u,
                  peer_v, ln2_g, ln2_b):
    mod = (jax.nn.silu(c) @ w_mod + b_mod).astype(jnp.float32)
    sh1, sc1, g1, sh2, sc2, g2 = jnp.split(mod, 6, axis=-1)
    h1 = layer_norm(x) * (1.0 + sc1[:, None]) + sh1[:, None]
    t_out, mstate, pool_state = token_mixer(h1, C0, n0, m0, pool_prefix, pos0, w_in, b_in, b_fgate, gn_gain,
                                            w_pool, pool_scale, w_branch_a, w_branch_b, w_out)
    x1 = layer_norm(DEEPNORM_ALPHA * x.astype(jnp.float32) + g1[:, None] * t_out) * ln1_g + ln1_b
    h2 = layer_norm(x1) * (1.0 + sc2[:, None]) + sh2[:, None]
    f_out = peer_ffn(h2, w_peer_q, peer_subkeys, peer_u, peer_v)
    x2 = layer_norm(DEEPNORM_ALPHA * x1 + g2[:, None] * f_out) * ln2_g + ln2_b
    return x2, mstate, pool_state


def setup_inputs(seed: int = 0) -> dict:
    key = jax.random.key(seed)
    ks = jax.random.split(key, 32)
    f32 = jnp.float32

    def nrm(k, shape, s):
        return jax.random.normal(k, shape, f32) * s

    L = DEPTH
    return {
        'x_prompt': nrm(ks[0], (BATCH, SEQ, D_MODEL), 1.0),
        'x_sample': nrm(ks[1], (DEC_BATCH, DEC_SEQ, D_MODEL), 1.0),
        'c_prompt': nrm(ks[2], (BATCH, D_MODEL), 1.0),
        'c_sample': nrm(ks[3], (DEC_BATCH, D_MODEL), 1.0),
        'state_mlstm_C': nrm(ks[4], (L, DEC_BATCH, MLSTM_HEADS, MLSTM_DK, MLSTM_DV), MLSTM_DK ** -0.5),
        'state_mlstm_n': nrm(ks[5], (L, DEC_BATCH, MLSTM_HEADS, MLSTM_DK), MLSTM_DK ** -0.5),
        'state_mlstm_m': nrm(ks[6], (L, DEC_BATCH, MLSTM_HEADS), 1.0),
        'state_pool': nrm(ks[7], (L, DEC_BATCH, POOL_BUF, D_B), 1.0),
        'w_mod': nrm(ks[8], (L, D_MODEL, 6 * D_MODEL), 0.5 * D_MODEL ** -0.5),
        'b_mod': nrm(ks[9], (L, 6 * D_MODEL), 0.02),
        'w_in': nrm(ks[10], (L, D_MODEL, N_IN), D_MODEL ** -0.5),
        'b_in': nrm(ks[11], (L, N_IN), 0.02),
        'b_fgate': jnp.linspace(3.0, 6.0, MLSTM_HEADS, dtype=f32)[None, :] + nrm(ks[12], (L, MLSTM_HEADS), 0.1),
        'gn_gain': 1.0 + nrm(ks[13], (L, D_A), 0.05),
        'w_pool': nrm(ks[14], (L, N_POOL_GROUPS, POOL_GROUP, POOL_GROUP), POOL_GROUP ** -0.5),
        'pool_scale': 1.0 + nrm(ks[15], (L, D_B), 0.1),
        'w_branch_a': nrm(ks[16], (L, D_A, D_MODEL), D_A ** -0.5),
        'w_branch_b': nrm(ks[17], (L, D_B, D_MODEL), D_B ** -0.5),
        'w_out': nrm(ks[18], (L, D_MODEL, D_MODEL), DEEPNORM_BETA * D_MODEL ** -0.5),
        'ln1_g': 1.0 + nrm(ks[19], (L, D_MODEL), 0.05),
        'ln1_b': nrm(ks[20], (L, D_MODEL), 0.02),
        'w_peer_q': nrm(ks[21], (L, D_MODEL, PEER_HEADS * PEER_DKEY), D_MODEL ** -0.5),
        'peer_subkeys': nrm(ks[22], (L, 2, PEER_N_KEYS, PEER_HALF), PEER_HALF ** -0.5),
        'peer_u': nrm(ks[23], (L, PEER_N_EXPERTS, D_MODEL), D_MODEL ** -0.5),
        'peer_v': nrm(ks[24], (L, PEER_N_EXPERTS, D_MODEL), DEEPNORM_BETA * PEER_HEADS ** -0.5),
        'ln2_g': 1.0 + nrm(ks[25], (L, D_MODEL), 0.05),
        'ln2_b': nrm(ks[26], (L, D_MODEL), 0.02),
    }


def reference(x_prompt, x_sample, c_prompt, c_sample, state_mlstm_C, state_mlstm_n, state_mlstm_m, state_pool,
              w_mod, b_mod, w_in, b_in, b_fgate, gn_gain, w_pool, pool_scale, w_branch_a, w_branch_b, w_out,
              ln1_g, ln1_b, w_peer_q, peer_subkeys, peer_u, peer_v, ln2_g, ln2_b):
    bp = x_prompt.shape[0]
    yp = x_prompt
    ys = x_sample
    Cp_l, np_l, mp_l, pp_l, Cs_l, ns_l, ms_l, ps_l = [], [], [], [], [], [], [], []
    for l in range(DEPTH):
        lw = (w_mod[l], b_mod[l], w_in[l], b_in[l], b_fgate[l], gn_gain[l], w_pool[l], pool_scale[l],
              w_branch_a[l], w_branch_b[l], w_out[l], ln1_g[l], ln1_b[l], w_peer_q[l], peer_subkeys[l],
              peer_u[l], peer_v[l], ln2_g[l], ln2_b[l])
        C0 = jnp.zeros((bp, MLSTM_HEADS, MLSTM_DK, MLSTM_DV), jnp.float32)
        n0 = jnp.zeros((bp, MLSTM_HEADS, MLSTM_DK), jnp.float32)
        m0 = jnp.zeros((bp, MLSTM_HEADS), jnp.float32)
        pool0 = jnp.zeros((bp, POOL_BUF, D_B), jnp.float32)
        yp, (Cp, npp, mp), pp = decoder_layer(yp, c_prompt, C0, n0, m0, pool0, 0, *lw)
        ys, (Cs, ns, ms), ps = decoder_layer(ys, c_sample, state_mlstm_C[l], state_mlstm_n[l], state_mlstm_m[l],
                                             state_pool[l], PAST_LEN, *lw)
        Cp_l.append(Cp); np_l.append(npp); mp_l.append(mp); pp_l.append(pp)
        Cs_l.append(Cs); ns_l.append(ns); ms_l.append(ms); ps_l.append(ps)
    y_prompt = yp.astype(x_prompt.dtype)
    y_sample = ys.astype(x_sample.dtype)
    C_prompt = jnp.stack(Cp_l)
    n_prompt = jnp.stack(np_l)
    m_prompt = jnp.stack(mp_l)
    pool_prompt = jnp.stack(pp_l)
    C_sample = jnp.stack(Cs_l)
    n_sample = jnp.stack(ns_l)
    m_sample = jnp.stack(ms_l)
    pool_sample = jnp.stack(ps_l)
    return (y_prompt, y_sample, C_prompt, n_prompt, m_prompt, pool_prompt, C_sample, n_sample, m_sample, pool_sample)
```

```python
import functools

import jax
import jax.numpy as jnp
from jax import lax
from jax.experimental import pallas as pl
from jax.experimental.pallas import tpu as pltpu

F32 = jnp.float32
BF16 = jnp.bfloat16

D_MODEL = 1024
MLSTM_HEADS = 4
MLSTM_DK = 128
MLSTM_DV = 128
D_A = MLSTM_HEADS * MLSTM_DV
QK_A = MLSTM_HEADS * MLSTM_DK
MLSTM_CHUNK = 128
POOL_WINDOWS = (2, 4, 8, 16)
N_POOL_GROUPS = 4
D_B = 512
POOL_GROUP = D_B // N_POOL_GROUPS
POOL_BUF = max(POOL_WINDOWS) - 1
PEER_HEADS = 8
PEER_HALF = 128
PEER_DKEY = 2 * PEER_HALF
PEER_N_KEYS = 128
PEER_N_EXPERTS = PEER_N_KEYS ** 2
PEER_TOPK = 16
OFF_Q = 0
OFF_K = OFF_Q + QK_A
OFF_V = OFF_K + QK_A
OFF_O = OFF_V + D_A
OFF_I = OFF_O + D_A
OFF_F = OFF_I + MLSTM_HEADS
OFF_U = OFF_F + MLSTM_HEADS
OFF_GA = OFF_U + D_B
OFF_GB = OFF_GA + D_MODEL
N_IN = OFF_GB + D_MODEL
DEPTH = 1
DEEPNORM_ALPHA = (2 * DEPTH) ** 0.25
LN_EPS = 1e-5
PAST_LEN = 16384

LANES = 128
SUBLANES = 8
VMEM_LIMIT_BYTES = 56 * 1024 * 1024

N_QKV = 3 * QK_A
N_REST = D_A + D_B + 2 * D_MODEL
N_GATE_PAD = LANES
REST_O, REST_U, REST_GA, REST_GB = 0, D_A, D_A + D_B, D_A + D_B + D_MODEL

N_RANKS = PEER_TOPK + 1
CAND_PAIRS = tuple((i, j) for i in range(N_RANKS) for j in range(N_RANKS) if (i + 1) * (j + 1) <= N_RANKS)
N_CAND_ROWS = -(-len(CAND_PAIRS) // SUBLANES) * SUBLANES


def _params(*sem):
    return pltpu.CompilerParams(dimension_semantics=sem, vmem_limit_bytes=VMEM_LIMIT_BYTES)


def _ln(x):
    mu = jnp.mean(x, axis=-1, keepdims=True)
    xc = x - mu
    var = jnp.mean(xc * xc, axis=-1, keepdims=True)
    return xc * lax.rsqrt(var + LN_EPS)


def _log_sigmoid(x):
    return jnp.minimum(x, 0.0) - jnp.log1p(jnp.exp(-jnp.abs(x)))


def _gelu_exact(x):
    return 0.5 * x * (1.0 + lax.erf(x * (0.5 ** 0.5)))


def _bdot(a, b):
    return jnp.dot(a.astype(BF16), b.astype(BF16), preferred_element_type=F32)


def _rows(ref):
    v = ref[...]
    return v.reshape(v.shape[-2], v.shape[-1])


def _mod_kernel(c_ref, w_ref, b_ref, o_ref):
    c = c_ref[...]
    o_ref[...] = _bdot(c * jax.nn.sigmoid(c), w_ref[...]) + b_ref[...]


def _modulation(c_all, w_mod, b_mod):
    n = c_all.shape[0]
    nblk = w_mod.shape[1] // D_MODEL
    return pl.pallas_call(
        _mod_kernel,
        grid=(nblk,),
        in_specs=[pl.BlockSpec((n, D_MODEL), lambda j: (0, 0)),
                  pl.BlockSpec((D_MODEL, D_MODEL), lambda j: (0, j)),
                  pl.BlockSpec((1, D_MODEL), lambda j: (0, j))],
        out_specs=pl.BlockSpec((n, D_MODEL), lambda j: (0, j)),
        out_shape=jax.ShapeDtypeStruct((n, w_mod.shape[1]), F32),
        compiler_params=_params("arbitrary"),
        name="modulation",
    )(c_all, w_mod, b_mod.reshape(1, -1))


def _mod_spec(per_batch, tm, tiles_per_batch, k):
    if per_batch:
        return pl.BlockSpec((1, 1, D_MODEL), lambda i: (i // tiles_per_batch, 0, k))
    return pl.BlockSpec((tm, D_MODEL), lambda i: (i, k))


def _in_proj_kernel(x_ref, sh_ref, sc_ref, w_ref, b_ref, qkv_ref, rest_ref, gate_ref):
    h1 = (_ln(x_ref[...]) * (1.0 + _rows(sc_ref)) + _rows(sh_ref)).astype(BF16)

    def proj(lo, hi):
        return jnp.dot(h1, w_ref[:, lo:hi], preferred_element_type=F32) + b_ref[:, lo:hi]

    qkv_ref[:, 0:QK_A] = proj(0, QK_A).astype(qkv_ref.dtype)
    qkv_ref[:, QK_A:2 * QK_A] = (proj(QK_A, 2 * QK_A) * (MLSTM_DK ** -0.5)).astype(qkv_ref.dtype)
    qkv_ref[:, 2 * QK_A:N_QKV] = proj(2 * QK_A, N_QKV).astype(qkv_ref.dtype)
    rest_ref[...] = proj(N_QKV, N_QKV + N_REST)
    gate_ref[...] = proj(N_QKV + N_REST, N_QKV + N_REST + N_GATE_PAD)


def _in_proj(x2d, mod, per_batch, tm, tiles_per_batch, w_r, b_r, qkv_dtype):
    n = x2d.shape[0]
    ncol = w_r.shape[1]
    return pl.pallas_call(
        _in_proj_kernel,
        grid=(n // tm,),
        in_specs=[pl.BlockSpec((tm, D_MODEL), lambda i: (i, 0)),
                  _mod_spec(per_batch, tm, tiles_per_batch, 0),
                  _mod_spec(per_batch, tm, tiles_per_batch, 1),
                  pl.BlockSpec((D_MODEL, ncol), lambda i: (0, 0)),
                  pl.BlockSpec((1, ncol), lambda i: (0, 0))],
        out_specs=[pl.BlockSpec((tm, N_QKV), lambda i: (i, 0)),
                   pl.BlockSpec((tm, N_REST), lambda i: (i, 0)),
                   pl.BlockSpec((tm, N_GATE_PAD), lambda i: (i, 0))],
        out_shape=[jax.ShapeDtypeStruct((n, N_QKV), qkv_dtype),
                   jax.ShapeDtypeStruct((n, N_REST), F32),
                   jax.ShapeDtypeStruct((n, N_GATE_PAD), F32)],
        compiler_params=_params("arbitrary"),
        name="in_proj",
    )(x2d, mod, mod, w_r, b_r)


def _mlstm_prompt_kernel(qkv_ref, gate_ref, o_ref, bf_ref, gn_ref, ha_ref, s_ref, m_ref):
    T = MLSTM_CHUNK

    @pl.when(pl.program_id(1) == 0)
    def _():
        s_ref[...] = jnp.zeros_like(s_ref)
        m_ref[...] = jnp.zeros_like(m_ref)

    col = gate_ref[...]
    lfc = _log_sigmoid(col + bf_ref[...])
    r_i = lax.broadcasted_iota(jnp.int32, (T, T), 0)
    c_i = lax.broadcasted_iota(jnp.int32, (T, T), 1)
    causal = r_i >= c_i
    tril = jnp.where(causal, 1.0, 0.0).astype(F32)
    bc = jnp.dot(tril, lfc, preferred_element_type=F32, precision=lax.Precision.HIGHEST)
    rows_t = jnp.where(c_i < MLSTM_HEADS, col, bc).T
    ones_col = jnp.where(c_i == 0, 1.0, 0.0).astype(BF16)

    for h in range(MLSTM_HEADS):
        q = qkv_ref[:, h * MLSTM_DK:(h + 1) * MLSTM_DK]
        k = qkv_ref[:, QK_A + h * MLSTM_DK:QK_A + (h + 1) * MLSTM_DK]
        v = qkv_ref[:, 2 * QK_A + h * MLSTM_DV:2 * QK_A + (h + 1) * MLSTM_DV]
        b_col = bc[:, MLSTM_HEADS + h:MLSTM_HEADS + h + 1]
        ig_col = col[:, h:h + 1]
        b_row = rows_t[MLSTM_HEADS + h:MLSTM_HEADS + h + 1, :]
        ig_row = rows_t[h:h + 1, :]
        m_prev = m_ref[0, h:h + 1, 0:1]
        state = s_ref[0, h]

        log_d = jnp.where(causal, b_col - b_row + ig_row, -jnp.inf)
        log_g = b_col + m_prev
        m_t = jnp.maximum(log_g, jnp.max(log_d, axis=-1, keepdims=True))
        scores = lax.dot_general(q, k, (((1,), (1,)), ((), ())), preferred_element_type=F32)
        w = jnp.exp(log_d - m_t) * scores
        w_state = jnp.exp(log_g - m_t)
        q_state = _bdot(q, state)
        num = _bdot(w, v) + w_state * q_state[:, :MLSTM_DV]
        den = jnp.sum(w, axis=-1, keepdims=True) + w_state * q_state[:, MLSTM_DV:MLSTM_DV + 1]
        hh = num / jnp.maximum(jnp.abs(den), jnp.exp(-m_t))
        hn = _ln(hh) * gn_ref[:, h * MLSTM_DV:(h + 1) * MLSTM_DV]
        gate_o = jax.nn.sigmoid(o_ref[:, h * MLSTM_DV:(h + 1) * MLSTM_DV])
        ha_ref[:, h * MLSTM_DV:(h + 1) * MLSTM_DV] = (hn * gate_o).astype(ha_ref.dtype)

        m_new = m_t[T - 1:T, :]
        b_last = b_col[T - 1:T, :]
        decay_s = jnp.exp(b_last - b_col + ig_col - m_new)
        decay_c = jnp.exp(b_last + m_prev - m_new)
        ks_t = (k.astype(F32) * decay_s).T.astype(BF16)
        v_ext = jnp.concatenate([v, ones_col], axis=1)
        s_ref[0, h] = decay_c * state + jnp.dot(ks_t, v_ext, preferred_element_type=F32)
        m_ref[0, h:h + 1, :] = jnp.broadcast_to(m_new, (1, LANES))


def _mlstm_prompt(qkv, gates, rest, bf_row, gn_row, batch, seq):
    nc = seq // MLSTM_CHUNK
    T = MLSTM_CHUNK
    return pl.pallas_call(
        _mlstm_prompt_kernel,
        grid=(batch, nc),
        in_specs=[pl.BlockSpec((T, N_QKV), lambda b, c: (b * nc + c, 0)),
                  pl.BlockSpec((T, N_GATE_PAD), lambda b, c: (b * nc + c, 0)),
                  pl.BlockSpec((T, D_A), lambda b, c: (b * nc + c, REST_O // D_A)),
                  pl.BlockSpec((1, N_GATE_PAD), lambda b, c: (0, 0)),
                  pl.BlockSpec((1, D_A), lambda b, c: (0, 0))],
        out_specs=[pl.BlockSpec((T, D_A), lambda b, c: (b * nc + c, 0)),
                   pl.BlockSpec((1, MLSTM_HEADS, MLSTM_DK, 2 * MLSTM_DV), lambda b, c: (b, 0, 0, 0)),
                   pl.BlockSpec((1, SUBLANES, LANES), lambda b, c: (b, 0, 0))],
        out_shape=[jax.ShapeDtypeStruct((batch * seq, D_A), BF16),
                   jax.ShapeDtypeStruct((batch, MLSTM_HEADS, MLSTM_DK, 2 * MLSTM_DV), F32),
                   jax.ShapeDtypeStruct((batch, SUBLANES, LANES), F32)],
        compiler_params=_params("arbitrary", "arbitrary"),
        name="mlstm_prompt",
    )(qkv, gates, rest, bf_row, gn_row)


def _mlstm_sample_kernel(qkv_ref, gate_ref, o_ref, bf_ref, gn_ref, c0_ref, n0_ref, m0_ref,
                         ha_ref, c_ref, n_ref, m_ref):
    nb = qkv_ref.shape[0]
    ig = gate_ref[...]
    lf = pltpu.roll(_log_sigmoid(ig + bf_ref[...]), LANES - MLSTM_HEADS, 1)
    m0 = m0_ref[...]
    m_t = jnp.maximum(lf + m0, ig)
    w_state_all = jnp.exp(lf + m0 - m_t)
    decay_s_all = jnp.exp(ig - m_t)
    floor_all = jnp.exp(-m_t)
    m_ref[...] = m_t
    pad = jnp.zeros((LANES - nb, MLSTM_DK), F32)

    for h in range(MLSTM_HEADS):
        qh = qkv_ref[:, h * MLSTM_DK:(h + 1) * MLSTM_DK]
        kh = qkv_ref[:, QK_A + h * MLSTM_DK:QK_A + (h + 1) * MLSTM_DK]
        vh = qkv_ref[:, 2 * QK_A + h * MLSTM_DV:2 * QK_A + (h + 1) * MLSTM_DV]
        q_t = jnp.concatenate([qh, pad], axis=0).T
        k_t = jnp.concatenate([kh, pad], axis=0).T
        gn = gn_ref[:, h * MLSTM_DV:(h + 1) * MLSTM_DV]
        gate_o = jax.nn.sigmoid(o_ref[:, h * MLSTM_DV:(h + 1) * MLSTM_DV])
        for j in range(nb):
            q_row, k_row, v_row = qh[j:j + 1, :], kh[j:j + 1, :], vh[j:j + 1, :]
            q_col, k_col = q_t[:, j:j + 1], k_t[:, j:j + 1]
            ws = w_state_all[j:j + 1, h:h + 1]
            ds = decay_s_all[j:j + 1, h:h + 1]
            cm = c0_ref[j, h]
            n_row = n0_ref[j, h:h + 1, :]
            q_c = jnp.sum(cm * q_col, axis=0, keepdims=True)
            qk = jnp.sum(q_row * k_row, axis=-1, keepdims=True)
            qn = jnp.sum(q_row * n_row, axis=-1, keepdims=True)
            wgt = ds * qk
            num = wgt * v_row + ws * q_c
            den = wgt + ws * qn
            hh = num / jnp.maximum(jnp.abs(den), floor_all[j:j + 1, h:h + 1])
            ha_ref[j:j + 1, h * MLSTM_DV:(h + 1) * MLSTM_DV] = _ln(hh) * gn * gate_o[j:j + 1, :]
            c_ref[j, h] = ws * cm + (ds * k_col) * v_row
            n_ref[j, h:h + 1, :] = ws * n_row + ds * k_row


def _mlstm_sample(qkv, gates, rest, bf_row, gn_row, c0, n0, m0):
    n = qkv.shape[0]
    nb = SUBLANES
    return pl.pallas_call(
        _mlstm_sample_kernel,
        grid=(n // nb,),
        in_specs=[pl.BlockSpec((nb, N_QKV), lambda i: (i, 0)),
                  pl.BlockSpec((nb, N_GATE_PAD), lambda i: (i, 0)),
                  pl.BlockSpec((nb, D_A), lambda i: (i, REST_O // D_A)),
                  pl.BlockSpec((1, N_GATE_PAD), lambda i: (0, 0)),
                  pl.BlockSpec((1, D_A), lambda i: (0, 0)),
                  pl.BlockSpec((nb, MLSTM_HEADS, MLSTM_DK, MLSTM_DV), lambda i: (i, 0, 0, 0)),
                  pl.BlockSpec((nb, MLSTM_HEADS, MLSTM_DK), lambda i: (i, 0, 0)),
                  pl.BlockSpec((nb, LANES), lambda i: (i, 0))],
        out_specs=[pl.BlockSpec((nb, D_A), lambda i: (i, 0)),
                   pl.BlockSpec((nb, MLSTM_HEADS, MLSTM_DK, MLSTM_DV), lambda i: (i, 0, 0, 0)),
                   pl.BlockSpec((nb, MLSTM_HEADS, MLSTM_DK), lambda i: (i, 0, 0)),
                   pl.BlockSpec((nb, LANES), lambda i: (i, 0))],
        out_shape=[jax.ShapeDtypeStruct((n, D_A), F32),
                   jax.ShapeDtypeStruct(c0.shape, F32),
                   jax.ShapeDtypeStruct(n0.shape, F32),
                   jax.ShapeDtypeStruct((n, LANES), F32)],
        compiler_params=_params("arbitrary"),
        name="mlstm_sample",
    )(qkv, gates, rest, bf_row, gn_row, c0, n0, m0)


def _pool_project(pooled, g, wp_ref, ps_ref):
    lo, hi = g * POOL_GROUP, (g + 1) * POOL_GROUP
    return _bdot(pooled, wp_ref[g]) * ps_ref[:, lo:hi]


def _pool_prompt_kernel(u_ref, wp_ref, ps_ref, pb_ref, ext_ref, *, rows):
    seq = u_ref.shape[0]
    halo = POOL_BUF + 1
    ext_ref[0:halo, :] = jnp.zeros((halo, D_B), F32)
    ext_ref[halo:halo + seq, :] = u_ref[...]

    for base in range(0, seq, rows):
        pos = base + lax.broadcasted_iota(jnp.int32, (rows, 1), 0)
        for g, wdw in enumerate(POOL_WINDOWS):
            lo, hi = g * POOL_GROUP, (g + 1) * POOL_GROUP
            cur = ext_ref[base + halo:base + halo + rows, lo:hi]
            acc = cur
            for s in range(1, wdw):
                acc = acc + ext_ref[base + halo - s:base + halo - s + rows, lo:hi]
            cnt = jnp.minimum(wdw, pos + 1).astype(F32)
            pooled = acc / cnt - cur
            pb_ref[base:base + rows, lo:hi] = _pool_project(pooled, g, wp_ref, ps_ref).astype(pb_ref.dtype)


def _pool_prompt(rest, w_pool, ps_row, batch, seq):
    rows = 256
    return pl.pallas_call(
        functools.partial(_pool_prompt_kernel, rows=rows),
        grid=(batch,),
        in_specs=[pl.BlockSpec((seq, D_B), lambda b: (b, REST_U // D_B)),
                  pl.BlockSpec((N_POOL_GROUPS, POOL_GROUP, POOL_GROUP), lambda b: (0, 0, 0)),
                  pl.BlockSpec((1, D_B), lambda b: (0, 0))],
        out_specs=pl.BlockSpec((seq, D_B), lambda b: (b, 0)),
        out_shape=jax.ShapeDtypeStruct((batch * seq, D_B), BF16),
        scratch_shapes=[pltpu.VMEM((seq + POOL_BUF + 1, D_B), F32)],
        compiler_params=_params("arbitrary"),
        name="pool_prompt",
    )(rest, w_pool, ps_row)


def _pool_sample_kernel(u_ref, st_ref, wp_ref, ps_ref, pb_ref):
    for g, wdw in enumerate(POOL_WINDOWS):
        lo, hi = g * POOL_GROUP, (g + 1) * POOL_GROUP
        cur = u_ref[:, lo:hi]
        acc = cur
        for s in range(1, wdw):
            acc = acc + st_ref[:, POOL_BUF - s, lo:hi]
        cnt = float(min(wdw, PAST_LEN + 1))
        pooled = acc / cnt - cur
        pb_ref[:, lo:hi] = _pool_project(pooled, g, wp_ref, ps_ref).astype(pb_ref.dtype)


def _pool_sample(rest, state_pool, w_pool, ps_row):
    n = rest.shape[0]
    nb = 32
    return pl.pallas_call(
        _pool_sample_kernel,
        grid=(n // nb,),
        in_specs=[pl.BlockSpec((nb, D_B), lambda i: (i, REST_U // D_B)),
                  pl.BlockSpec((nb, POOL_BUF, D_B), lambda i: (i, 0, 0)),
                  pl.BlockSpec((N_POOL_GROUPS, POOL_GROUP, POOL_GROUP), lambda i: (0, 0, 0)),
                  pl.BlockSpec((1, D_B), lambda i: (0, 0))],
        out_specs=pl.BlockSpec((nb, D_B), lambda i: (i, 0)),
        out_shape=jax.ShapeDtypeStruct((n, D_B), BF16),
        compiler_params=_params("arbitrary"),
        name="pool_sample",
    )(rest, state_pool, w_pool, ps_row)


def _merge_kernel(ha_ref, pb_ref, ga_ref, gb_ref, x_ref, g1_ref, sh2_ref, sc2_ref,
                  wa_ref, wb_ref, wo_ref, l1g_ref, l1b_ref, wq_ref, sk_ref,
                  x1_ref, h2t_ref, s_ref):
    tm = x_ref.shape[0]
    merged = (jax.nn.sigmoid(ga_ref[...]) * _bdot(ha_ref[...], wa_ref[...])
              + jax.nn.sigmoid(gb_ref[...]) * _bdot(pb_ref[...], wb_ref[...]))
    t_out = _bdot(merged, wo_ref[...])
    x1 = _ln(DEEPNORM_ALPHA * x_ref[...] + _rows(g1_ref) * t_out) * l1g_ref[...] + l1b_ref[...]
    x1_ref[...] = x1
    h2 = _ln(x1) * (1.0 + _rows(sc2_ref)) + _rows(sh2_ref)
    h2t_ref[...] = h2.T.astype(BF16)
    qp = _bdot(h2, wq_ref[...]).astype(BF16)
    for tb in range(tm // LANES):
        for h in range(PEER_HEADS):
            for p in range(2):
                lo = (h * 2 + p) * PEER_HALF
                qhp = qp[tb * LANES:(tb + 1) * LANES, lo:lo + PEER_HALF]
                s_ref[tb, h, p] = lax.dot_general(sk_ref[p], qhp, (((1,), (1,)), ((), ())),
                                                  preferred_element_type=F32)


def _merge(ha, pb, rest, x2d, mod, per_batch, tm, tiles_per_batch, wa, wb, wo, l1g, l1b, wq, sk):
    n = x2d.shape[0]
    const2 = lambda i: (0, 0)
    return pl.pallas_call(
        _merge_kernel,
        grid=(n // tm,),
        in_specs=[pl.BlockSpec((tm, D_A), lambda i: (i, 0)),
                  pl.BlockSpec((tm, D_B), lambda i: (i, 0)),
                  pl.BlockSpec((tm, D_MODEL), lambda i: (i, REST_GA // D_MODEL)),
                  pl.BlockSpec((tm, D_MODEL), lambda i: (i, REST_GB // D_MODEL)),
                  pl.BlockSpec((tm, D_MODEL), lambda i: (i, 0)),
                  _mod_spec(per_batch, tm, tiles_per_batch, 2),
                  _mod_spec(per_batch, tm, tiles_per_batch, 3),
                  _mod_spec(per_batch, tm, tiles_per_batch, 4),
                  pl.BlockSpec(wa.shape, const2),
                  pl.BlockSpec(wb.shape, const2),
                  pl.BlockSpec(wo.shape, const2),
                  pl.BlockSpec(l1g.shape, const2),
                  pl.BlockSpec(l1b.shape, const2),
                  pl.BlockSpec(wq.shape, const2),
                  pl.BlockSpec(sk.shape, lambda i: (0, 0, 0))],
        out_specs=[pl.BlockSpec((tm, D_MODEL), lambda i: (i, 0)),
                   pl.BlockSpec((D_MODEL, tm), lambda i: (0, i)),
                   pl.BlockSpec((tm // LANES, PEER_HEADS, 2, PEER_N_KEYS, LANES), lambda i: (i, 0, 0, 0, 0))],
        out_shape=[jax.ShapeDtypeStruct((n, D_MODEL), F32),
                   jax.ShapeDtypeStruct((D_MODEL, n), BF16),
                   jax.ShapeDtypeStruct((n // LANES, PEER_HEADS, 2, PEER_N_KEYS, LANES), F32)],
        compiler_params=_params("arbitrary"),
        name="merge",
    )(ha, pb, rest, rest, x2d, mod, mod, mod, wa, wb, wo, l1g, l1b, wq, sk)


def _top_rows(s, count):
    rows = []
    for _ in range(count):
        m = jnp.max(s, axis=0, keepdims=True)
        rows.append(m)
        s = jnp.where(s == m, -jnp.inf, s)
    return rows


def _route_kernel(s_ref, st_ref, cand_ref):
    cand_ref[...] = jnp.full(cand_ref.shape, -jnp.inf, F32)

    def body(h, carry):
        top1 = _top_rows(s_ref[0, h, 0], N_RANKS)
        top2 = _top_rows(s_ref[0, h, 1], N_RANKS)
        for r, (i, j) in enumerate(CAND_PAIRS):
            cand_ref[r:r + 1, :] = top1[i] + top2[j]
        best = _top_rows(cand_ref[...], N_RANKS)
        z = jnp.ones_like(best[0])
        for r in range(1, PEER_TOPK):
            z = z + jnp.exp(best[r] - best[0])
        st_ref[0, 0, pl.ds(h, 1), :] = 0.5 * (best[PEER_TOPK - 1] + best[PEER_TOPK])
        st_ref[0, 1, pl.ds(h, 1), :] = top1[0]
        st_ref[0, 2, pl.ds(h, 1), :] = top2[0]
        st_ref[0, 3, pl.ds(h, 1), :] = 1.0 / z
        return carry

    lax.fori_loop(0, PEER_HEADS, body, 0)


def _route(scores):
    ng = scores.shape[0]
    return pl.pallas_call(
        _route_kernel,
        grid=(ng,),
        in_specs=[pl.BlockSpec((1,) + scores.shape[1:], lambda i: (i, 0, 0, 0, 0))],
        out_specs=pl.BlockSpec((1, 4, PEER_HEADS, LANES), lambda i: (i, 0, 0, 0)),
        out_shape=jax.ShapeDtypeStruct((ng, 4, PEER_HEADS, LANES), F32),
        scratch_shapes=[pltpu.VMEM((N_CAND_ROWS, LANES), F32)],
        compiler_params=_params("arbitrary"),
        name="peer_route",
    )(scores)


def _peer_kernel(xt_ref, s_ref, st_ref, u_ref, vt_ref, yt_ref, h_scr, w_scr, e2_scr):
    j = pl.program_id(1)
    ntb = xt_ref.shape[1] // LANES
    n_a = u_ref.shape[0] // PEER_N_KEYS

    @pl.when(j == 0)
    def _():
        yt_ref[...] = jnp.zeros_like(yt_ref)
        for tb in range(ntb):
            for h in range(PEER_HEADS):
                e2_scr[tb, h] = jnp.exp(s_ref[tb, h, 1] - st_ref[tb, 2, h:h + 1, :]) * st_ref[tb, 3, h:h + 1, :]

    h_scr[...] = jnp.dot(u_ref[...], xt_ref[...], preferred_element_type=F32)

    def body(al, carry):
        a = j * n_a + al
        row0 = pl.multiple_of(al * PEER_N_KEYS, PEER_N_KEYS)
        for tb in range(ntb):
            act = _gelu_exact(h_scr[pl.ds(row0, PEER_N_KEYS), tb * LANES:(tb + 1) * LANES])
            gate = jnp.zeros((PEER_N_KEYS, LANES), F32)
            for h in range(PEER_HEADS):
                s1_row = s_ref[tb, h, 0, pl.ds(a, 1), :]
                thr = st_ref[tb, 0, h:h + 1, :] - s1_row
                e1 = jnp.exp(s1_row - st_ref[tb, 1, h:h + 1, :])
                gate = gate + jnp.where(s_ref[tb, h, 1] > thr, e2_scr[tb, h] * e1, 0.0)
            w_scr[pl.ds(row0, PEER_N_KEYS), tb * LANES:(tb + 1) * LANES] = (gate * act).astype(BF16)
        return carry

    lax.fori_loop(0, n_a, body, 0)
    yt_ref[...] += jnp.dot(vt_ref[...], w_scr[...], preferred_element_type=F32)


def _peer(xt, scores, stats, u_bf, vt_bf, tn, ec):
    n = xt.shape[1]
    ntb = tn // LANES
    return pl.pallas_call(
        _peer_kernel,
        grid=(n // tn, PEER_N_EXPERTS // ec),
        in_specs=[pl.BlockSpec((D_MODEL, tn), lambda i, j: (0, i)),
                  pl.BlockSpec((ntb,) + scores.shape[1:], lambda i, j: (i, 0, 0, 0, 0)),
                  pl.BlockSpec((ntb,) + stats.shape[1:], lambda i, j: (i, 0, 0, 0)),
                  pl.BlockSpec((ec, D_MODEL), lambda i, j: (j, 0)),
                  pl.BlockSpec((D_MODEL, ec), lambda i, j: (0, j))],
        out_specs=pl.BlockSpec((D_MODEL, tn), lambda i, j: (0, i)),
        out_shape=jax.ShapeDtypeStruct((D_MODEL, n), F32),
        scratch_shapes=[pltpu.VMEM((ec, tn), F32),
                        pltpu.VMEM((ec, tn), BF16),
                        pltpu.VMEM((ntb, PEER_HEADS, PEER_N_KEYS, LANES), F32)],
        compiler_params=_params("arbitrary", "arbitrary"),
        name="peer_dense",
    )(xt, scores, stats, u_bf, vt_bf)


def _final_kernel(x1_ref, yt_ref, g2_ref, l2g_ref, l2b_ref, o_ref):
    f_out = yt_ref[...].T
    o_ref[...] = _ln(DEEPNORM_ALPHA * x1_ref[...] + _rows(g2_ref) * f_out) * l2g_ref[...] + l2b_ref[...]


def _final(x1, yt, mod, per_batch, tm, tiles_per_batch, l2g, l2b):
    n = x1.shape[0]
    return pl.pallas_call(
        _final_kernel,
        grid=(n // tm,),
        in_specs=[pl.BlockSpec((tm, D_MODEL), lambda i: (i, 0)),
                  pl.BlockSpec((D_MODEL, tm), lambda i: (0, i)),
                  _mod_spec(per_batch, tm, tiles_per_batch, 5),
                  pl.BlockSpec(l2g.shape, lambda i: (0, 0)),
                  pl.BlockSpec(l2b.shape, lambda i: (0, 0))],
        out_specs=pl.BlockSpec((tm, D_MODEL), lambda i: (i, 0)),
        out_shape=jax.ShapeDtypeStruct((n, D_MODEL), F32),
        compiler_params=_params("arbitrary"),
        name="final_norm",
    )(x1, yt, mod, l2g, l2b)


def _layer_tail(ha, pb, rest, x2d, mod, per_batch, tm, tiles_per_batch, wts, peer_tn):
    x1, h2t, scores = _merge(ha, pb, rest, x2d, mod, per_batch, tm, tiles_per_batch,
                             wts["wa"], wts["wb"], wts["wo"], wts["l1g"], wts["l1b"], wts["wq"], wts["sk"])
    stats = _route(scores)
    yt = _peer(h2t, scores, stats, wts["u"], wts["vt"], peer_tn, 1024)
    return _final(x1, yt, mod, per_batch, tm, tiles_per_batch, wts["l2g"], wts["l2b"])


def kernel(x_prompt, x_sample, c_prompt, c_sample, state_mlstm_C, state_mlstm_n, state_mlstm_m, state_pool,
           w_mod, b_mod, w_in, b_in, b_fgate, gn_gain, w_pool, pool_scale, w_branch_a, w_branch_b, w_out,
           ln1_g, ln1_b, w_peer_q, peer_subkeys, peer_u, peer_v, ln2_g, ln2_b):
    assert w_mod.shape[0] == DEPTH
    batch, seq, _ = x_prompt.shape
    n_s = x_sample.shape[0]
    layer = 0
    row = lambda a: a.reshape(1, -1)

    w_in_l, b_in_l = w_in[layer], b_in[layer]
    gate_pad = N_GATE_PAD - 2 * MLSTM_HEADS
    w_r = jnp.concatenate([w_in_l[:, OFF_Q:OFF_I], w_in_l[:, OFF_U:N_IN], w_in_l[:, OFF_I:OFF_U],
                           jnp.zeros((D_MODEL, gate_pad), F32)], axis=1).astype(BF16)
    b_r = row(jnp.concatenate([b_in_l[OFF_Q:OFF_I], b_in_l[OFF_U:N_IN], b_in_l[OFF_I:OFF_U],
                               jnp.zeros((gate_pad,), F32)]))
    bf_row = row(jnp.concatenate([jnp.zeros((MLSTM_HEADS,), F32), b_fgate[layer],
                                  jnp.zeros((gate_pad,), F32)]))
    gn_row = row(gn_gain[layer])
    ps_row = row(pool_scale[layer])
    wts = dict(wa=w_branch_a[layer].astype(BF16), wb=w_branch_b[layer].astype(BF16), wo=w_out[layer].astype(BF16),
               l1g=row(ln1_g[layer]), l1b=row(ln1_b[layer]), wq=w_peer_q[layer].astype(BF16),
               sk=peer_subkeys[layer].astype(BF16), u=peer_u[layer].astype(BF16),
               vt=peer_v[layer].T.astype(BF16), l2g=row(ln2_g[layer]), l2b=row(ln2_b[layer]))
    w_pool_l = w_pool[layer].astype(BF16)

    mod = _modulation(jnp.concatenate([c_prompt, c_sample], axis=0), w_mod[layer], b_mod[layer])
    mod_p = mod[:batch].reshape(batch, 1, -1)
    mod_s = mod[batch:]

    tm_p = 256
    tpb = seq // tm_p
    xp = x_prompt.reshape(batch * seq, D_MODEL)
    qkv_p, rest_p, gates_p = _in_proj(xp, mod_p, True, tm_p, tpb, w_r, b_r, BF16)
    ha_p, s_p, m_p = _mlstm_prompt(qkv_p, gates_p, rest_p, bf_row, gn_row, batch, seq)
    pb_p = _pool_prompt(rest_p, w_pool_l, ps_row, batch, seq)
    y_p = _layer_tail(ha_p, pb_p, rest_p, xp, mod_p, True, tm_p, tpb, wts, 512)

    tm_s = n_s
    xs = x_sample.reshape(n_s, D_MODEL)
    qkv_s, rest_s, gates_s = _in_proj(xs, mod_s, False, tm_s, 1, w_r, b_r, F32)
    m0_pad = jnp.pad(state_mlstm_m[layer], ((0, 0), (0, LANES - MLSTM_HEADS)))
    ha_s, c_s, nrm_s, m_s = _mlstm_sample(qkv_s, gates_s, rest_s, bf_row, gn_row,
                                          state_mlstm_C[layer], state_mlstm_n[layer], m0_pad)
    m_s = m_s[:, :MLSTM_HEADS]
    pb_s = _pool_sample(rest_s, state_pool[layer], w_pool_l, ps_row)
    y_s = _layer_tail(ha_s, pb_s, rest_s, xs, mod_s, False, tm_s, 1, wts, n_s)

    u_p = rest_p[:, REST_U:REST_U + D_B].reshape(batch, seq, D_B)
    u_s = rest_s[:, REST_U:REST_U + D_B]
    pool_s = jnp.concatenate([state_pool[layer][:, 1:], u_s[:, None, :]], axis=1)
    return (y_p.reshape(x_prompt.shape), y_s.reshape(x_sample.shape),
            s_p[None, :, :, :, :MLSTM_DV], s_p[None, :, :, :, MLSTM_DV], m_p[None, :, :MLSTM_HEADS, 0],
            u_p[None, :, seq - POOL_BUF:], c_s[None], nrm_s[None], m_s[None], pool_s[None])
```
